```python
import math
import jax, jax.numpy as jnp
from jax import lax
import numpy as np

D_MODEL = 1024
BATCH = 8
SEQ = 4096
DEPTH = 1

PLE_DIM = 256
D_MIX = D_MODEL
CONV_WIDTH = D_MIX // 2
CONV_HEADS = 8
POOL_WIDTH = D_MIX - CONV_WIDTH
POOL_WINDOWS = (2, 4, 8, 16)
POOL_GROUP = POOL_WIDTH // len(POOL_WINDOWS)
IN_PROJ_WIDTH = 3 * CONV_WIDTH + POOL_WIDTH
CONV_K = 3
PEER_HEADS = 8
PEER_QDIM = 256
PEER_HALF = PEER_QDIM // 2
N_KEYS = 128
N_EXPERTS = N_KEYS * N_KEYS
PEER_TOPK = 16
TOKEN_BLOCK = 128
EPS = 1e-6

kernel_name = "hybrid_conv_pool_peer_block"


def rms_norm(x, g):
    xf = x.astype(jnp.float32)
    y = xf * lax.rsqrt(jnp.mean(xf * xf, axis=-1, keepdims=True) + EPS)
    return (y * g.astype(jnp.float32)).astype(x.dtype)


def short_conv_mixer(b_gate, c_gate, v, conv_w):
    z = c_gate * v
    zp = jnp.pad(z, ((0, 0), (1, 1), (0, 0)))
    y = conv_w[0] * zp[:, :-2] + conv_w[1] * zp[:, 1:-1] + conv_w[2] * zp[:, 2:]
    return b_gate * y


def pool_mixer(u, pool_w, pool_scale):
    bsz, seq, _ = u.shape
    ug = u.astype(jnp.float32).reshape(bsz, seq, len(POOL_WINDOWS), POOL_GROUP)
    t = jnp.arange(seq)
    outs = []
    for g, w in enumerate(POOL_WINDOWS):
        xg = ug[:, :, g]
        cs = jnp.concatenate([jnp.zeros((bsz, 1, POOL_GROUP), jnp.float32),
                              jnp.cumsum(xg, axis=1)], axis=1)
        lo = jnp.clip(t - w // 2, 0, seq)
        hi = jnp.clip(t + w // 2, 0, seq)
        win_sum = jnp.take(cs, hi, axis=1) - jnp.take(cs, lo, axis=1)
        cnt = (hi - lo).astype(jnp.float32)[None, :, None]
        outs.append(win_sum / cnt - xg)
    pooled = jnp.stack(outs, axis=2).astype(u.dtype)
    y = jnp.einsum('bsgc,gcd->bsgd', pooled, pool_w)
    return y.reshape(bsz, seq, POOL_WIDTH) * pool_scale


def peer_block(xb, w_q, sub_keys, expert_u, expert_v):
    T = xb.shape[0]
    q = (xb @ w_q).reshape(T, PEER_HEADS, 2, PEER_HALF)
    s = jnp.einsum('thpd,hpkd->thpk', q, sub_keys)
    sv, si = lax.top_k(s, PEER_TOPK)
    cand = (sv[:, :, 0, :, None] + sv[:, :, 1, None, :]).reshape(T, PEER_HEADS, PEER_TOPK * PEER_TOPK)
    cidx = (si[:, :, 0, :, None] * N_KEYS + si[:, :, 1, None, :]).reshape(T, PEER_HEADS, PEER_TOPK * PEER_TOPK)
    top_s, pos = lax.top_k(cand, PEER_TOPK)
    eidx = jnp.take_along_axis(cidx, pos, axis=-1)
    gate = jax.nn.softmax(top_s.astype(jnp.float32), axis=-1).astype(xb.dtype)
    u = jnp.take(expert_u, eidx, axis=0)
    act = jax.nn.gelu(jnp.einsum('thkd,td->thk', u, xb))
    v = jnp.take(expert_v, eidx, axis=0)
    return jnp.einsum('thk,thkd->td', gate * act, v)


def peer_ffn(xn, w_q, sub_keys, expert_u, expert_v):
    bsz, seq, d = xn.shape
    blocks = xn.reshape(-1, TOKEN_BLOCK, d)
    out = lax.map(lambda xb: peer_block(xb, w_q, sub_keys, expert_u, expert_v), blocks)
    return out.reshape(bsz, seq, d)


def setup_inputs(seed: int = 0) -> dict:
    key = jax.random.key(seed)
    ks = jax.random.split(key, 20)
    f32 = jnp.float32
    def nrm(k, shape, scale):
        return jax.random.normal(k, shape, f32) * scale
    def gain(k, shape):
        return 1.0 + 0.05 * jax.random.normal(k, shape, f32)
    return {
        "x": nrm(ks[0], (BATCH, SEQ, D_MODEL), 1.0),
        "p": nrm(ks[1], (DEPTH, BATCH, SEQ, PLE_DIM), 1.0),
        "g_mix": gain(ks[2], (DEPTH, D_MODEL)),
        "w_in": nrm(ks[3], (DEPTH, D_MODEL, IN_PROJ_WIDTH), D_MODEL ** -0.5),
        "conv_w": nrm(ks[4], (DEPTH, CONV_K, CONV_WIDTH), CONV_K ** -0.5),
        "pool_w": nrm(ks[5], (DEPTH, len(POOL_WINDOWS), POOL_GROUP, POOL_GROUP), POOL_GROUP ** -0.5),
        "pool_scale": gain(ks[6], (DEPTH, POOL_WIDTH)),
        "w_o": nrm(ks[7], (DEPTH, D_MIX, D_MODEL), D_MIX ** -0.5),
        "g_ffn": gain(ks[8], (DEPTH, D_MODEL)),
        "w_q": nrm(ks[9], (DEPTH, D_MODEL, PEER_HEADS * PEER_QDIM), D_MODEL ** -0.5),
        "sub_keys": nrm(ks[10], (DEPTH, PEER_HEADS, 2, N_KEYS, PEER_HALF), PEER_HALF ** -0.5),
        "expert_u": nrm(ks[11], (DEPTH, N_EXPERTS, D_MODEL), D_MODEL ** -0.5),
        "expert_v": nrm(ks[12], (DEPTH, N_EXPERTS, D_MODEL), PEER_HEADS ** -0.5),
        "g_ple": gain(ks[13], (DEPTH, D_MODEL)),
        "w_ple_gate": nrm(ks[14], (DEPTH, D_MODEL, D_MODEL), D_MODEL ** -0.5),
        "w_ple_proj": nrm(ks[15], (DEPTH, PLE_DIM, D_MODEL), PLE_DIM ** -0.5),
        "g_final": gain(ks[16], (D_MODEL,)),
    }


def reference(x, p, g_mix, w_in, conv_w, pool_w, pool_scale, w_o, g_ffn, w_q, sub_keys,
              expert_u, expert_v, g_ple, w_ple_gate, w_ple_proj, g_final):
    h = x
    for i in range(DEPTH):
        xn = rms_norm(h, g_mix[i])
        proj = xn @ w_in[i]
        b_gate = proj[..., :CONV_WIDTH]
        c_gate = proj[..., CONV_WIDTH:2 * CONV_WIDTH]
        v_conv = proj[..., 2 * CONV_WIDTH:3 * CONV_WIDTH]
        u_pool = proj[..., 3 * CONV_WIDTH:]
        y_conv = short_conv_mixer(b_gate, c_gate, v_conv, conv_w[i])
        y_pool = pool_mixer(u_pool, pool_w[i], pool_scale[i])
        h = h + jnp.concatenate([y_conv, y_pool], axis=-1) @ w_o[i]
        h = h + peer_ffn(rms_norm(h, g_ffn[i]), w_q[i], sub_keys[i], expert_u[i], expert_v[i])
        gate = jax.nn.sigmoid(rms_norm(h, g_ple[i]) @ w_ple_gate[i])
        h = h + gate * (p[i] @ w_ple_proj[i])
    return rms_norm(h, g_final)
```

```python
import functools
import math

import jax
import jax.numpy as jnp
from jax import lax
from jax.experimental import pallas as pl
from jax.experimental.pallas import tpu as pltpu

EPS = 1e-6
CONV_WIDTH = 512
POOL_WINDOWS = (2, 4, 8, 16)
POOL_GROUP = 128
PEER_HEADS = 8
PEER_HALF = 128
N_KEYS = 128
TOPK = 16
HALO = 8
PAIRS = PEER_HEADS * TOPK
LANES = 128
SUBLANES = 8
VMEM_LIMIT = 56 * 1024 * 1024


def _rms(x, g):
    return x * lax.rsqrt(jnp.mean(x * x, axis=-1, keepdims=True) + EPS) * g


def _bdot(a, b):
    return jnp.dot(a.astype(jnp.bfloat16), b.astype(jnp.bfloat16),
                   preferred_element_type=jnp.float32)


def _mixer_kernel(seq, tb, xp_ref, x_ref, xn_ref, g_ref, win_ref, cw_ref, pw_ref, ps_ref,
                  wo_ref, o_ref):
    i = pl.program_id(0)
    pos0 = (i * tb) % seq
    x = x_ref[...]
    xcat = jnp.concatenate([xp_ref[...], x, xn_ref[...]], axis=0)
    proj = _bdot(_rms(xcat, g_ref[...]), win_ref[...])
    rows = lax.broadcasted_iota(jnp.int32, (tb + 2 * HALO, 1), 0)
    pos = pos0 - HALO + rows
    inside = (pos >= 0) & (pos < seq)
    c = CONV_WIDTH
    b_gate = proj[HALO:HALO + tb, 0:c]
    z = jnp.where(inside, proj[:, c:2 * c] * proj[:, 2 * c:3 * c], 0.0)
    cw = cw_ref[...]
    y_conv = b_gate * (cw[0:1] * z[HALO - 1:HALO - 1 + tb] + cw[1:2] * z[HALO:HALO + tb]
                       + cw[2:3] * z[HALO + 1:HALO + 1 + tb])
    u = jnp.where(inside, proj[:, 3 * c:4 * c], 0.0)
    tpos = pos[HALO:HALO + tb]
    pooled = []
    for g, w in enumerate(POOL_WINDOWS):
        r = w // 2
        ug = u[:, g * POOL_GROUP:(g + 1) * POOL_GROUP]
        win = ug[HALO - r:HALO - r + tb]
        for k in range(-r + 1, r):
            win = win + ug[HALO + k:HALO + k + tb]
        cnt = (jnp.minimum(tpos + r, seq) - jnp.maximum(tpos - r, 0)).astype(jnp.float32)
        pooled_g = win / cnt - ug[HALO:HALO + tb]
        pooled.append(_bdot(pooled_g, pw_ref[g]))
    y_pool = jnp.concatenate(pooled, axis=-1) * ps_ref[...]
    y = jnp.concatenate([y_conv, y_pool], axis=-1)
    o_ref[...] = x + _bdot(y, wo_ref[...])


def _mixer(x, g_mix, w_in, conv_w, pool_w, pool_scale, w_o, seq, tb):
    t, d = x.shape
    nb = tb // HALO
    last = t // HALO - 1
    full = lambda shape: pl.BlockSpec(shape, lambda i: (0,) * len(shape))
    return pl.pallas_call(
        functools.partial(_mixer_kernel, seq, tb),
        grid=(t // tb,),
        in_specs=[
            pl.BlockSpec((HALO, d), lambda i: (jnp.maximum(i * nb - 1, 0), 0)),
            pl.BlockSpec((tb, d), lambda i: (i, 0)),
            pl.BlockSpec((HALO, d), lambda i: (jnp.minimum((i + 1) * nb, last), 0)),
            full((1, d)), full(w_in.shape), full(conv_w.shape), full(pool_w.shape),
            full((1, pool_scale.shape[-1])), full(w_o.shape),
        ],
        out_specs=pl.BlockSpec((tb, d), lambda i: (i, 0)),
        out_shape=jax.ShapeDtypeStruct((t, d), jnp.float32),
        compiler_params=pltpu.CompilerParams(dimension_semantics=("arbitrary",),
                                             vmem_limit_bytes=VMEM_LIMIT),
        name="mixer",
    )(x, x, x, g_mix.reshape(1, d), w_in.astype(jnp.bfloat16), conv_w,
      pool_w.astype(jnp.bfloat16), pool_scale.reshape(1, -1), w_o.astype(jnp.bfloat16))


def _top16(s, payload=None):
    n = s.shape[0]
    rows = lax.broadcasted_iota(jnp.int32, s.shape, 0)
    vals, picks = [], []
    for _ in range(TOPK):
        m = jnp.max(s, axis=0, keepdims=True)
        idx = jnp.min(jnp.where(s == m, rows, n), axis=0, keepdims=True)
        hit = rows == idx
        vals.append(m)
        if payload is None:
            picks.append(idx)
        else:
            picks.append(jnp.max(jnp.where(hit, payload, -1), axis=0, keepdims=True))
        s = jnp.where(hit, -jnp.inf, s)
    return jnp.concatenate(vals, axis=0), jnp.concatenate(picks, axis=0)


def _route_kernel(h_ref, g_ref, wq_ref, keys_ref, xn_ref, idx_ref, gate_ref):
    xn = _rms(h_ref[...], g_ref[...])
    xn_ref[...] = xn
    q = _bdot(xn, wq_ref[...])
    tb = q.shape[0]
    for h in range(PEER_HEADS):
        sv, si = [], []
        for p in range(2):
            col = (h * 2 + p) * PEER_HALF
            s_t = lax.dot_general(keys_ref[h * 2 + p], q[:, col:col + PEER_HALF],
                                  (((1,), (1,)), ((), ())),
                                  preferred_element_type=jnp.float32)
            v, ix = _top16(s_t)
            sv.append(v)
            si.append(ix)
        cand = (sv[0][:, None, :] + sv[1][None, :, :]).reshape(TOPK * TOPK, tb)
        cidx = (si[0][:, None, :] * N_KEYS + si[1][None, :, :]).reshape(TOPK * TOPK, tb)
        top_s, eidx = _top16(cand, cidx)
        e = jnp.exp(top_s - top_s[0:1])
        gate = e / jnp.sum(e, axis=0, keepdims=True)
        idx_ref[h * TOPK:(h + 1) * TOPK, :] = eidx
        gate_ref[h * TOPK:(h + 1) * TOPK, :] = gate


def _route(h1, g_ffn, w_q, sub_keys, tb):
    t, d = h1.shape
    keys = sub_keys.reshape(PEER_HEADS * 2, N_KEYS, PEER_HALF)
    full = lambda shape: pl.BlockSpec(shape, lambda i: (0,) * len(shape))
    return pl.pallas_call(
        _route_kernel,
        grid=(t // tb,),
        in_specs=[pl.BlockSpec((tb, d), lambda i: (i, 0)), full((1, d)), full(w_q.shape),
                  full(keys.shape)],
        out_specs=[pl.BlockSpec((tb, d), lambda i: (i, 0)),
                   pl.BlockSpec((PAIRS, tb), lambda i: (0, i)),
                   pl.BlockSpec((PAIRS, tb), lambda i: (0, i))],
        out_shape=[jax.ShapeDtypeStruct((t, d), jnp.float32),
                   jax.ShapeDtypeStruct((PAIRS, t), jnp.int32),
                   jax.ShapeDtypeStruct((PAIRS, t), jnp.float32)],
        compiler_params=pltpu.CompilerParams(dimension_semantics=("arbitrary",),
                                             vmem_limit_bytes=VMEM_LIMIT),
        name="route",
    )(h1, g_ffn.reshape(1, d), w_q.astype(jnp.bfloat16), keys)


def _pack_kernel(w_ref, o_ref):
    o_ref[...] = w_ref[...].astype(jnp.bfloat16)


def _pack_table(w, rows=1024):
    n, d = w.shape
    w3 = w.reshape(n, d // LANES, LANES)
    return pl.pallas_call(
        _pack_kernel,
        grid=(n // rows,),
        in_specs=[pl.BlockSpec((rows, d // LANES, LANES), lambda i: (i, 0, 0))],
        out_specs=pl.BlockSpec((rows, d // LANES, LANES), lambda i: (i, 0, 0)),
        out_shape=jax.ShapeDtypeStruct(w3.shape, jnp.bfloat16),
        name="pack_table",
    )(w3)


def _score_kernel(tb, idx_ref, x_ref, gate_ref, tab_ref, c_ref, part_ref):
    ones = jnp.ones((SUBLANES, LANES), jnp.float32)

    def token(t, carry):
        x = x_ref[t]

        def pair(j, carry2):
            e = idx_ref[t, j]
            prod = tab_ref[e].astype(jnp.float32) * x
            part_ref[pl.ds(j, 1), :] = jnp.sum(prod, axis=0, keepdims=True)
            return carry2

        lax.fori_loop(0, PAIRS, pair, 0, unroll=8)
        s = lax.dot_general(ones, part_ref[...], (((1,), (1,)), ((), ())),
                            precision=lax.Precision.HIGHEST,
                            preferred_element_type=jnp.float32)[0:1]
        c_ref[pl.ds(t, 1), :] = gate_ref[pl.ds(t, 1), :] * jax.nn.gelu(s)
        return carry

    lax.fori_loop(0, tb, token, 0)


def _score(idx, xn, gate, table, tb):
    t, d = xn.shape
    x3 = xn.reshape(t, d // LANES, LANES)
    return pl.pallas_call(
        functools.partial(_score_kernel, tb),
        grid=(t // tb,),
        in_specs=[
            pl.BlockSpec((tb, PAIRS), lambda i: (i, 0), memory_space=pltpu.SMEM),
            pl.BlockSpec((tb, d // LANES, LANES), lambda i: (i, 0, 0)),
            pl.BlockSpec((tb, PAIRS), lambda i: (i, 0)),
            pl.BlockSpec(table.shape, lambda i: (0, 0, 0), pipeline_mode=pl.Buffered(1)),
        ],
        out_specs=pl.BlockSpec((tb, PAIRS), lambda i: (i, 0)),
        out_shape=jax.ShapeDtypeStruct((t, PAIRS), jnp.float32),
        scratch_shapes=[pltpu.VMEM((PAIRS, LANES), jnp.float32)],
        compiler_params=pltpu.CompilerParams(dimension_semantics=("arbitrary",),
                                             vmem_limit_bytes=VMEM_LIMIT),
        name="score",
    )(idx, x3, gate, table)


def _combine_kernel(tb, idx_ref, c_ref, tab_ref, o_ref):
    def token(t, carry):
        def pair(j, acc):
            e = idx_ref[t, j]
            return acc + c_ref[t, j] * tab_ref[e].astype(jnp.float32)

        acc = lax.fori_loop(0, PAIRS, pair, jnp.zeros((SUBLANES, LANES), jnp.float32),
                            unroll=8)
        o_ref[t] = acc
        return carry

    lax.fori_loop(0, tb, token, 0)


def _combine(idx, coef, table, tb):
    t = idx.shape[0]
    n, s, l = table.shape
    return pl.pallas_call(
        functools.partial(_combine_kernel, tb),
        grid=(t // tb,),
        in_specs=[
            pl.BlockSpec((tb, PAIRS), lambda i: (i, 0), memory_space=pltpu.SMEM),
            pl.BlockSpec((tb, PAIRS), lambda i: (i, 0), memory_space=pltpu.SMEM),
            pl.BlockSpec(table.shape, lambda i: (0, 0, 0), pipeline_mode=pl.Buffered(1)),
        ],
        out_specs=pl.BlockSpec((tb, s, l), lambda i: (i, 0, 0)),
        out_shape=jax.ShapeDtypeStruct((t, s, l), jnp.float32),
        compiler_params=pltpu.CompilerParams(dimension_semantics=("arbitrary",),
                                             vmem_limit_bytes=VMEM_LIMIT),
        name="combine",
    )(idx, coef, table)


def _ple_kernel(final_norm, h_ref, y_ref, p_ref, g_ref, wg_ref, wp_ref, gf_ref, o_ref):
    h = h_ref[...] + y_ref[...]
    gate = jax.nn.sigmoid(_bdot(_rms(h, g_ref[...]), wg_ref[...]))
    h = h + gate * _bdot(p_ref[...], wp_ref[...])
    o_ref[...] = _rms(h, gf_ref[...]) if final_norm else h


def _ple(h1, y, p, g_ple, w_gate, w_proj, g_final, final_norm, tb):
    t, d = h1.shape
    full = lambda shape: pl.BlockSpec(shape, lambda i: (0,) * len(shape))
    return pl.pallas_call(
        functools.partial(_ple_kernel, final_norm),
        grid=(t // tb,),
        in_specs=[pl.BlockSpec((tb, d), lambda i: (i, 0)), pl.BlockSpec((tb, d), lambda i: (i, 0)),
                  pl.BlockSpec((tb, p.shape[-1]), lambda i: (i, 0)), full((1, d)),
                  full(w_gate.shape), full(w_proj.shape), full((1, d))],
        out_specs=pl.BlockSpec((tb, d), lambda i: (i, 0)),
        out_shape=jax.ShapeDtypeStruct((t, d), jnp.float32),
        compiler_params=pltpu.CompilerParams(dimension_semantics=("arbitrary",),
                                             vmem_limit_bytes=VMEM_LIMIT),
        name="ple",
    )(h1, y, p, g_ple.reshape(1, d), w_gate.astype(jnp.bfloat16), w_proj.astype(jnp.bfloat16),
      g_final.reshape(1, d))


def kernel(x, p, g_mix, w_in, conv_w, pool_w, pool_scale, w_o, g_ffn, w_q, sub_keys, expert_u,
           expert_v, g_ple, w_ple_gate, w_ple_proj, g_final):
    bsz, seq, d = x.shape
    t = bsz * seq
    h = x.reshape(t, d)
    for i in range(p.shape[0]):
        h1 = _mixer(h, g_mix[i], w_in[i], conv_w[i], pool_w[i], pool_scale[i], w_o[i], seq,
                    tb=min(512, seq))
        xn, idx_t, gate_t = _route(h1, g_ffn[i], w_q[i], sub_keys[i], tb=min(256, t))
        idx = idx_t.T
        coef = _score(idx, xn, gate_t.T, _pack_table(expert_u[i]), tb=min(128, t))
        y = _combine(idx, coef, _pack_table(expert_v[i]), tb=min(128, t))
        h = _ple(h1, y.reshape(t, d), p[i].reshape(t, -1), g_ple[i], w_ple_gate[i],
                 w_ple_proj[i], g_final, final_norm=(i == p.shape[0] - 1), tb=min(512, t))
    return h.reshape(bsz, seq, d)
```

```python
import functools
import math

import jax
import jax.numpy as jnp
from jax import lax
from jax.experimental import pallas as pl
from jax.experimental.pallas import tpu as pltpu

EPS = 1e-6
CONV_WIDTH = 512
POOL_WINDOWS = (2, 4, 8, 16)
POOL_GROUP = 128
PEER_HEADS = 8
PEER_HALF = 128
N_KEYS = 128
TOPK = 16
HALO = 8
PAIRS = PEER_HEADS * TOPK
LANES = 128
SUBLANES = 8
VMEM_LIMIT = 56 * 1024 * 1024


def _rms(x, g):
    return x * lax.rsqrt(jnp.mean(x * x, axis=-1, keepdims=True) + EPS) * g


def _bdot(a, b):
    return jnp.dot(a.astype(jnp.bfloat16), b.astype(jnp.bfloat16),
                   preferred_element_type=jnp.float32)


def _mixer_kernel(seq, tb, xp_ref, x_ref, xn_ref, g_ref, win_ref, cw_ref, pw_ref, ps_ref,
                  wo_ref, o_ref):
    i = pl.program_id(0)
    pos0 = (i * tb) % seq
    x = x_ref[...]
    xcat = jnp.concatenate([xp_ref[...], x, xn_ref[...]], axis=0)
    proj = _bdot(_rms(xcat, g_ref[...]), win_ref[...])
    rows = lax.broadcasted_iota(jnp.int32, (tb + 2 * HALO, 1), 0)
    pos = pos0 - HALO + rows
    inside = (pos >= 0) & (pos < seq)
    c = CONV_WIDTH
    b_gate = proj[HALO:HALO + tb, 0:c]
    z = jnp.where(inside, proj[:, c:2 * c] * proj[:, 2 * c:3 * c], 0.0)
    cw = cw_ref[...]
    y_conv = b_gate * (cw[0:1] * z[HALO - 1:HALO - 1 + tb] + cw[1:2] * z[HALO:HALO + tb]
                       + cw[2:3] * z[HALO + 1:HALO + 1 + tb])
    u = jnp.where(inside, proj[:, 3 * c:4 * c], 0.0)
    tpos = pos[HALO:HALO + tb]
    pooled = []
    for g, w in enumerate(POOL_WINDOWS):
        r = w // 2
        ug = u[:, g * POOL_GROUP:(g + 1) * POOL_GROUP]
        win = ug[HALO - r:HALO - r + tb]
        for k in range(-r + 1, r):
            win = win + ug[HALO + k:HALO + k + tb]
        cnt = (jnp.minimum(tpos + r, seq) - jnp.maximum(tpos - r, 0)).astype(jnp.float32)
        pooled_g = win / cnt - ug[HALO:HALO + tb]
        pooled.append(_bdot(pooled_g, pw_ref[g]))
    y_pool = jnp.concatenate(pooled, axis=-1) * ps_ref[...]
    y = jnp.concatenate([y_conv, y_pool], axis=-1)
    o_ref[...] = x + _bdot(y, wo_ref[...])


def _mixer(x, g_mix, w_in, conv_w, pool_w, pool_scale, w_o, seq, tb):
    t, d = x.shape
    nb = tb // HALO
    last = t // HALO - 1
    full = lambda shape: pl.BlockSpec(shape, lambda i: (0,) * len(shape))
    return pl.pallas_call(
        functools.partial(_mixer_kernel, seq, tb),
        grid=(t // tb,),
        in_specs=[
            pl.BlockSpec((HALO, d), lambda i: (jnp.maximum(i * nb - 1, 0), 0)),
            pl.BlockSpec((tb, d), lambda i: (i, 0)),
            pl.BlockSpec((HALO, d), lambda i: (jnp.minimum((i + 1) * nb, last), 0)),
            full((1, d)), full(w_in.shape), full(conv_w.shape), full(pool_w.shape),
            full((1, pool_scale.shape[-1])), full(w_o.shape),
        ],
        out_specs=pl.BlockSpec((tb, d), lambda i: (i, 0)),
        out_shape=jax.ShapeDtypeStruct((t, d), jnp.float32),
        compiler_params=pltpu.CompilerParams(dimension_semantics=("arbitrary",),
                                             vmem_limit_bytes=VMEM_LIMIT),
        name="mixer",
    )(x, x, x, g_mix.reshape(1, d), w_in.astype(jnp.bfloat16), conv_w,
      pool_w.astype(jnp.bfloat16), pool_scale.reshape(1, -1), w_o.astype(jnp.bfloat16))


def _top16(s, payload=None):
    n = s.shape[0]
    rows = lax.broadcasted_iota(jnp.int32, s.shape, 0)
    vals, picks = [], []
    for _ in range(TOPK):
        m = jnp.max(s, axis=0, keepdims=True)
        idx = jnp.min(jnp.where(s == m, rows, n), axis=0, keepdims=True)
        hit = rows == idx
        vals.append(m)
        if payload is None:
            picks.append(idx)
        else:
            picks.append(jnp.max(jnp.where(hit, payload, -1), axis=0, keepdims=True))
        s = jnp.where(hit, -jnp.inf, s)
    return jnp.concatenate(vals, axis=0), jnp.concatenate(picks, axis=0)


def _route_kernel(h_ref, g_ref, wq_ref, keys_ref, xn_ref, idx_ref, gate_ref):
    xn = _rms(h_ref[...], g_ref[...])
    xn_ref[...] = xn
    q = _bdot(xn, wq_ref[...])
    tb = q.shape[0]
    for h in range(PEER_HEADS):
        sv, si = [], []
        for p in range(2):
            col = (h * 2 + p) * PEER_HALF
            s_t = lax.dot_general(keys_ref[h * 2 + p], q[:, col:col + PEER_HALF],
                                  (((1,), (1,)), ((), ())),
                                  preferred_element_type=jnp.float32)
            v, ix = _top16(s_t)
            sv.append(v)
            si.append(ix)
        cand = (sv[0][:, None, :] + sv[1][None, :, :]).reshape(TOPK * TOPK, tb)
        cidx = (si[0][:, None, :] * N_KEYS + si[1][None, :, :]).reshape(TOPK * TOPK, tb)
        top_s, eidx = _top16(cand, cidx)
        e = jnp.exp(top_s - top_s[0:1])
        gate = e / jnp.sum(e, axis=0, keepdims=True)
        idx_ref[h * TOPK:(h + 1) * TOPK, :] = eidx
        gate_ref[h * TOPK:(h + 1) * TOPK, :] = gate


def _route(h1, g_ffn, w_q, sub_keys, tb):
    t, d = h1.shape
    keys = sub_keys.reshape(PEER_HEADS * 2, N_KEYS, PEER_HALF)
    full = lambda shape: pl.BlockSpec(shape, lambda i: (0,) * len(shape))
    return pl.pallas_call(
        _route_kernel,
        grid=(t // tb,),
        in_specs=[pl.BlockSpec((tb, d), lambda i: (i, 0)), full((1, d)), full(w_q.shape),
                  full(keys.shape)],
        out_specs=[pl.BlockSpec((tb, d), lambda i: (i, 0)),
                   pl.BlockSpec((PAIRS, tb), lambda i: (0, i)),
                   pl.BlockSpec((PAIRS, tb), lambda i: (0, i))],
        out_shape=[jax.ShapeDtypeStruct((t, d), jnp.float32),
                   jax.ShapeDtypeStruct((PAIRS, t), jnp.int32),
                   jax.ShapeDtypeStruct((PAIRS, t), jnp.float32)],
        compiler_params=pltpu.CompilerParams(dimension_semantics=("arbitrary",),
                                             vmem_limit_bytes=VMEM_LIMIT),
        name="route",
    )(h1, g_ffn.reshape(1, d), w_q.astype(jnp.bfloat16), keys)


def _pack_kernel(w_ref, o_ref):
    o_ref[...] = w_ref[...].astype(jnp.bfloat16)


def _pack_table(w, rows=1024):
    n, d = w.shape
    w3 = w.reshape(n, d // LANES, LANES)
    return pl.pallas_call(
        _pack_kernel,
        grid=(n // rows,),
        in_specs=[pl.BlockSpec((rows, d // LANES, LANES), lambda i: (i, 0, 0))],
        out_specs=pl.BlockSpec((rows, d // LANES, LANES), lambda i: (i, 0, 0)),
        out_shape=jax.ShapeDtypeStruct(w3.shape, jnp.bfloat16),
        name="pack_table",
    )(w3)


def _gather_tiles(idx_ref, t, tab_ref, g_ref):
    row_ref = idx_ref.at[pl.ds(t * PAIRS, PAIRS)]
    for j in range(PAIRS):
        g_ref[j * SUBLANES:(j + 1) * SUBLANES, :] = tab_ref[row_ref[j]]


def _chunk_mask():
    m = lax.broadcasted_iota(jnp.int32, (SUBLANES, PAIRS * SUBLANES), 0)
    n = lax.broadcasted_iota(jnp.int32, (SUBLANES, PAIRS * SUBLANES), 1)
    return (n % SUBLANES) == m


def _split_bf16(a):
    hi = a.astype(jnp.bfloat16)
    lo = (a - hi.astype(jnp.float32)).astype(jnp.bfloat16)
    return hi, lo


GROUP = 2
TILE_ROWS = PAIRS * SUBLANES


def _for_tokens(tb, idx_ref, tab_ref, g_ref, start, finish):
    def gather_group(first, slot_ref):
        for k in range(GROUP):
            t = jnp.minimum(first + k, tb - 1)
            _gather_tiles(idx_ref, t, tab_ref, slot_ref.at[pl.ds(k * TILE_ROWS, TILE_ROWS)])

    gather_group(0, g_ref.at[0])

    def group(i, carry):
        slot = i % 2
        cur, nxt = g_ref.at[slot], g_ref.at[1 - slot]
        outs = [start(i * GROUP + k, cur.at[pl.ds(k * TILE_ROWS, TILE_ROWS)])
                for k in range(GROUP)]
        gather_group((i + 1) * GROUP, nxt)
        for k in range(GROUP):
            finish(i * GROUP + k, outs[k])
        return carry

    lax.fori_loop(0, tb // GROUP, group, 0)


def _score_kernel(tb, idx_ref, x_ref, gate_ref, tab_ref, c_ref, g_ref, d_ref):
    mask = _chunk_mask()

    def start(t, g_ref):
        xb = x_ref[t].astype(jnp.bfloat16)
        return lax.dot_general(xb, g_ref[...], (((1,), (1,)), ((), ())),
                               preferred_element_type=jnp.float32)

    def finish(t, o):
        d_ref[pl.ds(pl.multiple_of(t * SUBLANES, SUBLANES), SUBLANES), :] = jnp.where(mask, o, 0.0)

    _for_tokens(tb, idx_ref, tab_ref, g_ref, start, finish)
    n = lax.broadcasted_iota(jnp.int32, (PAIRS * SUBLANES, PAIRS), 0)
    j = lax.broadcasted_iota(jnp.int32, (PAIRS * SUBLANES, PAIRS), 1)
    fold = (n // SUBLANES == j).astype(jnp.bfloat16)
    hi, lo = _split_bf16(d_ref[...])
    s8 = (jnp.dot(hi, fold, preferred_element_type=jnp.float32)
          + jnp.dot(lo, fold, preferred_element_type=jnp.float32))
    s = jnp.sum(s8.reshape(tb, SUBLANES, PAIRS), axis=1)
    c_ref[...] = gate_ref[...] * jax.nn.gelu(s)


def _score(idx_flat, xn, gate, table, tb):
    t, d = xn.shape
    x3 = xn.reshape(t, d // LANES, LANES)
    return pl.pallas_call(
        functools.partial(_score_kernel, tb),
        grid=(t // tb,),
        in_specs=[
            pl.BlockSpec((tb * PAIRS,), lambda i: (i,), memory_space=pltpu.SMEM),
            pl.BlockSpec((tb, d // LANES, LANES), lambda i: (i, 0, 0)),
            pl.BlockSpec((tb, PAIRS), lambda i: (i, 0)),
            pl.BlockSpec(table.shape, lambda i: (0, 0, 0), pipeline_mode=pl.Buffered(1)),
        ],
        out_specs=pl.BlockSpec((tb, PAIRS), lambda i: (i, 0)),
        out_shape=jax.ShapeDtypeStruct((t, PAIRS), jnp.float32),
        scratch_shapes=[pltpu.VMEM((2, GROUP * TILE_ROWS, LANES), jnp.bfloat16),
                        pltpu.VMEM((tb * SUBLANES, PAIRS * SUBLANES), jnp.float32)],
        compiler_params=pltpu.CompilerParams(dimension_semantics=("arbitrary",),
                                             vmem_limit_bytes=VMEM_LIMIT),
        name="score",
    )(idx_flat, x3, gate, table)


def _combine_kernel(tb, idx_ref, c_ref, tab_ref, o_ref, g_ref, ce_ref):
    mask = _chunk_mask()
    j = lax.broadcasted_iota(jnp.int32, (PAIRS, PAIRS * SUBLANES), 0)
    n = lax.broadcasted_iota(jnp.int32, (PAIRS, PAIRS * SUBLANES), 1)
    spread = (n // SUBLANES == j).astype(jnp.bfloat16)
    hi, lo = _split_bf16(c_ref[...])
    ce_ref[...] = (jnp.dot(hi, spread, preferred_element_type=jnp.float32)
                   + jnp.dot(lo, spread, preferred_element_type=jnp.float32))

    def start(t, g_ref):
        row = jnp.broadcast_to(ce_ref[pl.ds(t, 1), :], (SUBLANES, PAIRS * SUBLANES))
        lhs_hi, lhs_lo = _split_bf16(jnp.where(mask, row, 0.0))
        return jnp.dot(jnp.concatenate([lhs_hi, lhs_lo], axis=0), g_ref[...],
                       preferred_element_type=jnp.float32)

    def finish(t, o):
        o_ref[t] = o[0:SUBLANES] + o[SUBLANES:2 * SUBLANES]

    _for_tokens(tb, idx_ref, tab_ref, g_ref, start, finish)


def _combine(idx_flat, coef, table, tb):
    t = coef.shape[0]
    n, s, l = table.shape
    return pl.pallas_call(
        functools.partial(_combine_kernel, tb),
        grid=(t // tb,),
        in_specs=[
            pl.BlockSpec((tb * PAIRS,), lambda i: (i,), memory_space=pltpu.SMEM),
            pl.BlockSpec((tb, PAIRS), lambda i: (i, 0)),
            pl.BlockSpec(table.shape, lambda i: (0, 0, 0), pipeline_mode=pl.Buffered(1)),
        ],
        out_specs=pl.BlockSpec((tb, s, l), lambda i: (i, 0, 0)),
        out_shape=jax.ShapeDtypeStruct((t, s, l), jnp.float32),
        scratch_shapes=[pltpu.VMEM((2, GROUP * TILE_ROWS, LANES), jnp.bfloat16),
                        pltpu.VMEM((tb, PAIRS * SUBLANES), jnp.float32)],
        compiler_params=pltpu.CompilerParams(dimension_semantics=("arbitrary",),
                                             vmem_limit_bytes=VMEM_LIMIT),
        name="combine",
    )(idx_flat, coef, table)


def _ple_kernel(final_norm, h_ref, y_ref, p_ref, g_ref, wg_ref, wp_ref, gf_ref, o_ref):
    h = h_ref[...] + y_ref[...]
    gate = jax.nn.sigmoid(_bdot(_rms(h, g_ref[...]), wg_ref[...]))
    h = h + gate * _bdot(p_ref[...], wp_ref[...])
    o_ref[...] = _rms(h, gf_ref[...]) if final_norm else h


def _ple(h1, y, p, g_ple, w_gate, w_proj, g_final, final_norm, tb):
    t, d = h1.shape
    full = lambda shape: pl.BlockSpec(shape, lambda i: (0,) * len(shape))
    return pl.pallas_call(
        functools.partial(_ple_kernel, final_norm),
        grid=(t // tb,),
        in_specs=[pl.BlockSpec((tb, d), lambda i: (i, 0)), pl.BlockSpec((tb, d), lambda i: (i, 0)),
                  pl.BlockSpec((tb, p.shape[-1]), lambda i: (i, 0)), full((1, d)),
                  full(w_gate.shape), full(w_proj.shape), full((1, d))],
        out_specs=pl.BlockSpec((tb, d), lambda i: (i, 0)),
        out_shape=jax.ShapeDtypeStruct((t, d), jnp.float32),
        compiler_params=pltpu.CompilerParams(dimension_semantics=("arbitrary",),
                                             vmem_limit_bytes=VMEM_LIMIT),
        name="ple",
    )(h1, y, p, g_ple.reshape(1, d), w_gate.astype(jnp.bfloat16), w_proj.astype(jnp.bfloat16),
      g_final.reshape(1, d))


def kernel(x, p, g_mix, w_in, conv_w, pool_w, pool_scale, w_o, g_ffn, w_q, sub_keys, expert_u,
           expert_v, g_ple, w_ple_gate, w_ple_proj, g_final):
    bsz, seq, d = x.shape
    t = bsz * seq
    h = x.reshape(t, d)
    for i in range(p.shape[0]):
        h1 = _mixer(h, g_mix[i], w_in[i], conv_w[i], pool_w[i], pool_scale[i], w_o[i], seq,
                    tb=min(512, seq))
        xn, idx_t, gate_t = _route(h1, g_ffn[i], w_q[i], sub_keys[i], tb=min(256, t))
        idx = idx_t.T.reshape(t * PAIRS)
        coef = _score(idx, xn, gate_t.T, _pack_table(expert_u[i]), tb=min(128, t))
        y = _combine(idx, coef, _pack_table(expert_v[i]), tb=min(128, t))
        h = _ple(h1, y.reshape(t, d), p[i].reshape(t, -1), g_ple[i], w_ple_gate[i],
                 w_ple_proj[i], g_final, final_norm=(i == p.shape[0] - 1), tb=min(512, t))
    return h.reshape(bsz, seq, d)
```

```python
import functools

import jax
import jax.numpy as jnp
from jax import lax
from jax.experimental import pallas as pl
from jax.experimental.pallas import tpu as pltpu

EPS = 1e-6
CONV_WIDTH = 512
POOL_WINDOWS = (2, 4, 8, 16)
POOL_GROUP = 128
PEER_HEADS = 8
PEER_HALF = 128
N_KEYS = 128
TOPK = 16
HALO = 8
PAIRS = PEER_HEADS * TOPK
LANES = 128
SUBLANES = 8
TILE_WORDS = SUBLANES // 2
GROUP = 2
VMEM_LIMIT = 56 * 1024 * 1024


def _rms(x, g):
    return x * lax.rsqrt(jnp.mean(x * x, axis=-1, keepdims=True) + EPS) * g


def _bdot(a, b):
    return jnp.dot(a.astype(jnp.bfloat16), b.astype(jnp.bfloat16),
                   preferred_element_type=jnp.float32)


def _mixer_kernel(seq, tb, xp_ref, x_ref, xn_ref, g_ref, win_ref, cw_ref, pw_ref, ps_ref,
                  wo_ref, o_ref):
    i = pl.program_id(0)
    pos0 = (i * tb) % seq
    x = x_ref[...]
    xcat = jnp.concatenate([xp_ref[...], x, xn_ref[...]], axis=0)
    proj = _bdot(_rms(xcat, g_ref[...]), win_ref[...])
    rows = lax.broadcasted_iota(jnp.int32, (tb + 2 * HALO, 1), 0)
    pos = pos0 - HALO + rows
    inside = (pos >= 0) & (pos < seq)
    c = CONV_WIDTH
    b_gate = proj[HALO:HALO + tb, 0:c]
    z = jnp.where(inside, proj[:, c:2 * c] * proj[:, 2 * c:3 * c], 0.0)
    cw = cw_ref[...]
    y_conv = b_gate * (cw[0:1] * z[HALO - 1:HALO - 1 + tb] + cw[1:2] * z[HALO:HALO + tb]
                       + cw[2:3] * z[HALO + 1:HALO + 1 + tb])
    u = jnp.where(inside, proj[:, 3 * c:4 * c], 0.0)
    tpos = pos[HALO:HALO + tb]
    pooled = []
    for g, w in enumerate(POOL_WINDOWS):
        r = w // 2
        ug = u[:, g * POOL_GROUP:(g + 1) * POOL_GROUP]
        win = ug[HALO - r:HALO - r + tb]
        for k in range(-r + 1, r):
            win = win + ug[HALO + k:HALO + k + tb]
        cnt = (jnp.minimum(tpos + r, seq) - jnp.maximum(tpos - r, 0)).astype(jnp.float32)
        pooled_g = win / cnt - ug[HALO:HALO + tb]
        pooled.append(_bdot(pooled_g, pw_ref[g]))
    y_pool = jnp.concatenate(pooled, axis=-1) * ps_ref[...]
    y = jnp.concatenate([y_conv, y_pool], axis=-1)
    o_ref[...] = x + _bdot(y, wo_ref[...])


def _mixer(x, g_mix, w_in, conv_w, pool_w, pool_scale, w_o, seq, tb):
    t, d = x.shape
    nb = tb // HALO
    last = t // HALO - 1
    full = lambda shape: pl.BlockSpec(shape, lambda i: (0,) * len(shape))
    return pl.pallas_call(
        functools.partial(_mixer_kernel, seq, tb),
        grid=(t // tb,),
        in_specs=[
            pl.BlockSpec((HALO, d), lambda i: (jnp.maximum(i * nb - 1, 0), 0)),
            pl.BlockSpec((tb, d), lambda i: (i, 0)),
            pl.BlockSpec((HALO, d), lambda i: (jnp.minimum((i + 1) * nb, last), 0)),
            full((1, d)), full(w_in.shape), full(conv_w.shape), full(pool_w.shape),
            full((1, pool_scale.shape[-1])), full(w_o.shape),
        ],
        out_specs=pl.BlockSpec((tb, d), lambda i: (i, 0)),
        out_shape=jax.ShapeDtypeStruct((t, d), jnp.float32),
        compiler_params=pltpu.CompilerParams(dimension_semantics=("arbitrary",),
                                             vmem_limit_bytes=VMEM_LIMIT),
        name="mixer",
    )(x, x, x, g_mix.reshape(1, d), w_in.astype(jnp.bfloat16), conv_w,
      pool_w.astype(jnp.bfloat16), pool_scale.reshape(1, -1), w_o.astype(jnp.bfloat16))


def _top16(s, payload=None):
    n = s.shape[0]
    rows = lax.broadcasted_iota(jnp.int32, s.shape, 0)
    vals, picks = [], []
    for _ in range(TOPK):
        m = jnp.max(s, axis=0, keepdims=True)
        idx = jnp.min(jnp.where(s == m, rows, n), axis=0, keepdims=True)
        hit = rows == idx
        vals.append(m)
        if payload is None:
            picks.append(idx)
        else:
            picks.append(jnp.max(jnp.where(hit, payload, -1), axis=0, keepdims=True))
        s = jnp.where(hit, -jnp.inf, s)
    return jnp.concatenate(vals, axis=0), jnp.concatenate(picks, axis=0)


_STAIR = [TOPK // (a + 1) for a in range(TOPK)]
_STAIR_ROWS = -(-sum(_STAIR) // SUBLANES) * SUBLANES


def _route_kernel(h_ref, g_ref, wq_ref, keys_ref, xn_ref, idx_ref, gate_ref):
    xn = _rms(h_ref[...], g_ref[...])
    xn_ref[...] = xn
    q = _bdot(xn, wq_ref[...])
    tb = q.shape[0]
    pad = _STAIR_ROWS - sum(_STAIR)
    for h in range(PEER_HEADS):
        sv, si = [], []
        for p in range(2):
            col = (h * 2 + p) * PEER_HALF
            s_t = lax.dot_general(keys_ref[h * 2 + p], q[:, col:col + PEER_HALF],
                                  (((1,), (1,)), ((), ())),
                                  preferred_element_type=jnp.float32)
            v, ix = _top16(s_t)
            sv.append(v)
            si.append(ix)
        cand = jnp.concatenate([sv[0][a:a + 1] + sv[1][0:n] for a, n in enumerate(_STAIR)]
                               + [jnp.full((pad, tb), -jnp.inf, jnp.float32)], axis=0)
        cidx = jnp.concatenate([si[0][a:a + 1] * N_KEYS + si[1][0:n] for a, n in enumerate(_STAIR)]
                               + [jnp.zeros((pad, tb), jnp.int32)], axis=0)
        top_s, eidx = _top16(cand, cidx)
        e = jnp.exp(top_s - top_s[0:1])
        gate = e / jnp.sum(e, axis=0, keepdims=True)
        idx_ref[h * TOPK:(h + 1) * TOPK, :] = eidx * TILE_WORDS
        gate_ref[h * TOPK:(h + 1) * TOPK, :] = gate


def _route(h1, g_ffn, w_q, sub_keys, tb):
    t, d = h1.shape
    keys = sub_keys.reshape(PEER_HEADS * 2, N_KEYS, PEER_HALF)
    full = lambda shape: pl.BlockSpec(shape, lambda i: (0,) * len(shape))
    return pl.pallas_call(
        _route_kernel,
        grid=(t // tb,),
        in_specs=[pl.BlockSpec((tb, d), lambda i: (i, 0)), full((1, d)), full(w_q.shape),
                  full(keys.shape)],
        out_specs=[pl.BlockSpec((tb, d), lambda i: (i, 0)),
                   pl.BlockSpec((PAIRS, tb), lambda i: (0, i)),
                   pl.BlockSpec((PAIRS, tb), lambda i: (0, i))],
        out_shape=[jax.ShapeDtypeStruct((t, d), jnp.float32),
                   jax.ShapeDtypeStruct((PAIRS, t), jnp.int32),
                   jax.ShapeDtypeStruct((PAIRS, t), jnp.float32)],
        compiler_params=pltpu.CompilerParams(dimension_semantics=("arbitrary",),
                                             vmem_limit_bytes=VMEM_LIMIT),
        name="route",
    )(h1, g_ffn.reshape(1, d), w_q.astype(jnp.bfloat16), keys)


def _pack_kernel(w_ref, o_ref):
    o_ref[...] = pltpu.bitcast(w_ref[...].astype(jnp.bfloat16), jnp.uint32)


def _pack_table(w, experts_per_step=1024):
    n, d = w.shape
    assert d == SUBLANES * LANES
    rows = experts_per_step * SUBLANES
    return pl.pallas_call(
        _pack_kernel,
        grid=(n // experts_per_step,),
        in_specs=[pl.BlockSpec((rows, LANES), lambda i: (i, 0))],
        out_specs=pl.BlockSpec((rows // 2, LANES), lambda i: (i, 0)),
        out_shape=jax.ShapeDtypeStruct((n * TILE_WORDS, LANES), jnp.uint32),
        name="pack_table",
    )(w.reshape(n * SUBLANES, LANES))


def _load_tile(tab_ref, offset):
    words = tab_ref[pl.ds(pl.multiple_of(offset, TILE_WORDS), TILE_WORDS), :]
    return pltpu.bitcast(words, jnp.bfloat16).astype(jnp.float32)


def _split_bf16(a):
    hi = a.astype(jnp.bfloat16)
    lo = (a - hi.astype(jnp.float32)).astype(jnp.bfloat16)
    return hi, lo


def _row_sums(a):
    ones = jnp.ones((LANES, LANES), jnp.bfloat16)
    hi, lo = _split_bf16(a)
    return (jnp.dot(hi, ones, preferred_element_type=jnp.float32)
            + jnp.dot(lo, ones, preferred_element_type=jnp.float32))


def _eye():
    r = lax.broadcasted_iota(jnp.int32, (PAIRS, LANES), 0)
    c = lax.broadcasted_iota(jnp.int32, (PAIRS, LANES), 1)
    return r == c


def _fold(a, b, k, sub):
    lo = (sub & k) == 0
    if 2 * k == SUBLANES:
        return jnp.where(lo, a, b) + pltpu.roll(jnp.where(lo, b, a), k, axis=0)
    return (jnp.where(lo, a, pltpu.roll(b, k, axis=0))
            + jnp.where(lo, pltpu.roll(a, SUBLANES - k, axis=0), b))


def _sublane_sums(p, sub):
    q = [_fold(p[0], p[4], 4, sub), _fold(p[2], p[6], 4, sub),
         _fold(p[1], p[5], 4, sub), _fold(p[3], p[7], 4, sub)]
    return _fold(_fold(q[0], q[1], 2, sub), _fold(q[2], q[3], 2, sub), 1, sub)


def _score_kernel(tb, idx_ref, x_ref, gate_ref, tab_ref, c_ref, ra_ref, rb_ref, s_ref):
    sub = lax.broadcasted_iota(jnp.int32, (SUBLANES, LANES), 0)
    eye = _eye()

    def partials(first, r_ref):
        for k in range(GROUP):
            row_ref = idx_ref.at[pl.ds((first + k) * PAIRS, PAIRS)]
            x = x_ref[first + k]
            for g in range(PAIRS // SUBLANES):
                prods = [_load_tile(tab_ref, row_ref[g * SUBLANES + i]) * x
                         for i in range(SUBLANES)]
                r_ref[pl.ds(k * PAIRS + g * SUBLANES, SUBLANES), :] = _sublane_sums(prods, sub)

    def finish(first, z):
        for k in range(GROUP):
            zk = jnp.where(eye, z[k * PAIRS:(k + 1) * PAIRS], 0.0)
            parts = [zk[g * SUBLANES:(g + 1) * SUBLANES] for g in range(PAIRS // SUBLANES)]
            while len(parts) > 1:
                parts = [parts[n] + parts[n + 1] for n in range(0, len(parts), 2)]
            row = pl.multiple_of(jnp.maximum(first + k, 0) * SUBLANES, SUBLANES)
            s_ref[pl.ds(row, SUBLANES), :] = parts[0]

    rb_ref[...] = jnp.zeros((GROUP * PAIRS, LANES), jnp.float32)

    def two_groups(i, carry):
        first = 2 * i * GROUP
        z = _row_sums(rb_ref[...])
        partials(first, ra_ref)
        finish(first - GROUP, z)
        z = _row_sums(ra_ref[...])
        partials(first + GROUP, rb_ref)
        finish(first, z)
        return carry

    lax.fori_loop(0, tb // (2 * GROUP), two_groups, 0)
    finish(tb - GROUP, _row_sums(rb_ref[...]))
    s = jnp.sum(s_ref[...].reshape(tb, SUBLANES, PAIRS), axis=1)
    c_ref[...] = gate_ref[...] * jax.nn.gelu(s)


def _score(idx_flat, xn, gate, table, tb):
    t, d = xn.shape
    x3 = xn.reshape(t, SUBLANES, LANES)
    return pl.pallas_call(
        functools.partial(_score_kernel, tb),
        grid=(t // tb,),
        in_specs=[
            pl.BlockSpec((tb * PAIRS,), lambda i: (i,), memory_space=pltpu.SMEM),
            pl.BlockSpec((tb, SUBLANES, LANES), lambda i: (i, 0, 0)),
            pl.BlockSpec((tb, PAIRS), lambda i: (i, 0)),
            pl.BlockSpec(table.shape, lambda i: (0, 0), pipeline_mode=pl.Buffered(1)),
        ],
        out_specs=pl.BlockSpec((tb, PAIRS), lambda i: (i, 0)),
        out_shape=jax.ShapeDtypeStruct((t, PAIRS), jnp.float32),
        scratch_shapes=[pltpu.VMEM((GROUP * PAIRS, LANES), jnp.float32),
                        pltpu.VMEM((GROUP * PAIRS, LANES), jnp.float32),
                        pltpu.VMEM((tb * SUBLANES, PAIRS), jnp.float32)],
        compiler_params=pltpu.CompilerParams(dimension_semantics=("arbitrary",),
                                             vmem_limit_bytes=VMEM_LIMIT),
        name="score",
    )(idx_flat, x3, gate, table)


_ACCS = 4


def _combine_kernel(tb, idx_ref, c_ref, tab_ref, o_ref, cba_ref, cbb_ref):
    eye = _eye()

    def spread(first):
        rows = [jnp.where(eye, jnp.broadcast_to(c_ref[pl.ds(jnp.minimum(first + k, tb - 1), 1), :],
                                                (PAIRS, LANES)), 0.0) for k in range(GROUP)]
        return _row_sums(jnp.concatenate(rows, axis=0))

    def accumulate(first, cb_ref):
        for k in range(GROUP):
            row_ref = idx_ref.at[pl.ds((first + k) * PAIRS, PAIRS)]
            accs = [jnp.zeros((SUBLANES, LANES), jnp.float32) for _ in range(_ACCS)]
            for j in range(PAIRS):
                accs[j % _ACCS] = accs[j % _ACCS] + (cb_ref[pl.ds(k * PAIRS + j, 1), :]
                                                      * _load_tile(tab_ref, row_ref[j]))
            o_ref[first + k] = (accs[0] + accs[1]) + (accs[2] + accs[3])

    cba_ref[...] = spread(0)

    def two_groups(i, carry):
        first = 2 * i * GROUP
        nxt = spread(first + GROUP)
        accumulate(first, cba_ref)
        cbb_ref[...] = nxt
        nxt = spread(first + 2 * GROUP)
        accumulate(first + GROUP, cbb_ref)
        cba_ref[...] = nxt
        return carry

    lax.fori_loop(0, tb // (2 * GROUP), two_groups, 0)


def _combine(idx_flat, coef, table, tb):
    t = coef.shape[0]
    return pl.pallas_call(
        functools.partial(_combine_kernel, tb),
        grid=(t // tb,),
        in_specs=[
            pl.BlockSpec((tb * PAIRS,), lambda i: (i,), memory_space=pltpu.SMEM),
            pl.BlockSpec((tb, PAIRS), lambda i: (i, 0)),
            pl.BlockSpec(table.shape, lambda i: (0, 0), pipeline_mode=pl.Buffered(1)),
        ],
        out_specs=pl.BlockSpec((tb, SUBLANES, LANES), lambda i: (i, 0, 0)),
        out_shape=jax.ShapeDtypeStruct((t, SUBLANES, LANES), jnp.float32),
        scratch_shapes=[pltpu.VMEM((GROUP * PAIRS, LANES), jnp.float32),
                        pltpu.VMEM((GROUP * PAIRS, LANES), jnp.float32)],
        compiler_params=pltpu.CompilerParams(dimension_semantics=("arbitrary",),
                                             vmem_limit_bytes=VMEM_LIMIT),
        name="combine",
    )(idx_flat, coef, table)


def _ple_kernel(final_norm, h_ref, y_ref, p_ref, g_ref, wg_ref, wp_ref, gf_ref, o_ref):
    h = h_ref[...] + y_ref[...]
    gate = jax.nn.sigmoid(_bdot(_rms(h, g_ref[...]), wg_ref[...]))
    h = h + gate * _bdot(p_ref[...], wp_ref[...])
    o_ref[...] = _rms(h, gf_ref[...]) if final_norm else h


def _ple(h1, y, p, g_ple, w_gate, w_proj, g_final, final_norm, tb):
    t, d = h1.shape
    full = lambda shape: pl.BlockSpec(shape, lambda i: (0,) * len(shape))
    return pl.pallas_call(
        functools.partial(_ple_kernel, final_norm),
        grid=(t // tb,),
        in_specs=[pl.BlockSpec((tb, d), lambda i: (i, 0)), pl.BlockSpec((tb, d), lambda i: (i, 0)),
                  pl.BlockSpec((tb, p.shape[-1]), lambda i: (i, 0)), full((1, d)),
                  full(w_gate.shape), full(w_proj.shape), full((1, d))],
        out_specs=pl.BlockSpec((tb, d), lambda i: (i, 0)),
        out_shape=jax.ShapeDtypeStruct((t, d), jnp.float32),
        compiler_params=pltpu.CompilerParams(dimension_semantics=("arbitrary",),
                                             vmem_limit_bytes=VMEM_LIMIT),
        name="ple",
    )(h1, y, p, g_ple.reshape(1, d), w_gate.astype(jnp.bfloat16), w_proj.astype(jnp.bfloat16),
      g_final.reshape(1, d))


def kernel(x, p, g_mix, w_in, conv_w, pool_w, pool_scale, w_o, g_ffn, w_q, sub_keys, expert_u,
           expert_v, g_ple, w_ple_gate, w_ple_proj, g_final):
    bsz, seq, d = x.shape
    t = bsz * seq
    h = x.reshape(t, d)
    for i in range(p.shape[0]):
        h1 = _mixer(h, g_mix[i], w_in[i], conv_w[i], pool_w[i], pool_scale[i], w_o[i], seq,
                    tb=min(512, seq))
        xn, idx_t, gate_t = _route(h1, g_ffn[i], w_q[i], sub_keys[i], tb=min(128, t))
        idx = idx_t.T.reshape(t * PAIRS)
        coef = _score(idx, xn, gate_t.T, _pack_table(expert_u[i]), tb=min(128, t))
        y = _combine(idx, coef, _pack_table(expert_v[i]), tb=min(128, t))
        h = _ple(h1, y.reshape(t, d), p[i].reshape(t, -1), g_ple[i], w_ple_gate[i],
                 w_ple_proj[i], g_final, final_norm=(i == p.shape[0] - 1), tb=min(512, t))
    return h.reshape(bsz, seq, d)
```

```python
import functools

import jax
import jax.numpy as jnp
from jax import lax
from jax.experimental import pallas as pl
from jax.experimental.pallas import tpu as pltpu

EPS = 1e-6
CONV_WIDTH = 512
POOL_WINDOWS = (2, 4, 8, 16)
POOL_GROUP = 128
PEER_HEADS = 8
PEER_HALF = 128
N_KEYS = 128
TOPK = 16
HALO = 8
PAIRS = PEER_HEADS * TOPK
LANES = 128
SUBLANES = 8
TILE_WORDS = SUBLANES // 2
GROUP = 4
VMEM_LIMIT = 56 * 1024 * 1024


def _rms(x, g):
    return x * lax.rsqrt(jnp.mean(x * x, axis=-1, keepdims=True) + EPS) * g


def _bdot(a, b):
    return jnp.dot(a.astype(jnp.bfloat16), b.astype(jnp.bfloat16),
                   preferred_element_type=jnp.float32)


def _mixer_kernel(seq, tb, xp_ref, x_ref, xn_ref, g_ref, win_ref, cw_ref, pw_ref, ps_ref,
                  wo_ref, o_ref):
    i = pl.program_id(0)
    pos0 = (i * tb) % seq
    x = x_ref[...]
    xcat = jnp.concatenate([xp_ref[...], x, xn_ref[...]], axis=0)
    proj = _bdot(_rms(xcat, g_ref[...]), win_ref[...])
    rows = lax.broadcasted_iota(jnp.int32, (tb + 2 * HALO, 1), 0)
    pos = pos0 - HALO + rows
    inside = (pos >= 0) & (pos < seq)
    c = CONV_WIDTH
    b_gate = proj[HALO:HALO + tb, 0:c]
    z = jnp.where(inside, proj[:, c:2 * c] * proj[:, 2 * c:3 * c], 0.0)
    cw = cw_ref[...]
    y_conv = b_gate * (cw[0:1] * z[HALO - 1:HALO - 1 + tb] + cw[1:2] * z[HALO:HALO + tb]
                       + cw[2:3] * z[HALO + 1:HALO + 1 + tb])
    u = jnp.where(inside, proj[:, 3 * c:4 * c], 0.0)
    tpos = pos[HALO:HALO + tb]
    pooled = []
    for g, w in enumerate(POOL_WINDOWS):
        r = w // 2
        ug = u[:, g * POOL_GROUP:(g + 1) * POOL_GROUP]
        win = ug[HALO - r:HALO - r + tb]
        for k in range(-r + 1, r):
            win = win + ug[HALO + k:HALO + k + tb]
        cnt = (jnp.minimum(tpos + r, seq) - jnp.maximum(tpos - r, 0)).astype(jnp.float32)
        pooled_g = win / cnt - ug[HALO:HALO + tb]
        pooled.append(_bdot(pooled_g, pw_ref[g]))
    y_pool = jnp.concatenate(pooled, axis=-1) * ps_ref[...]
    y = jnp.concatenate([y_conv, y_pool], axis=-1)
    o_ref[...] = x + _bdot(y, wo_ref[...])


def _mixer(x, g_mix, w_in, conv_w, pool_w, pool_scale, w_o, seq, tb):
    t, d = x.shape
    nb = tb // HALO
    last = t // HALO - 1
    full = lambda shape: pl.BlockSpec(shape, lambda i: (0,) * len(shape))
    return pl.pallas_call(
        functools.partial(_mixer_kernel, seq, tb),
        grid=(t // tb,),
        in_specs=[
            pl.BlockSpec((HALO, d), lambda i: (jnp.maximum(i * nb - 1, 0), 0)),
            pl.BlockSpec((tb, d), lambda i: (i, 0)),
            pl.BlockSpec((HALO, d), lambda i: (jnp.minimum((i + 1) * nb, last), 0)),
            full((1, d)), full(w_in.shape), full(conv_w.shape), full(pool_w.shape),
            full((1, pool_scale.shape[-1])), full(w_o.shape),
        ],
        out_specs=pl.BlockSpec((tb, d), lambda i: (i, 0)),
        out_shape=jax.ShapeDtypeStruct((t, d), jnp.float32),
        compiler_params=pltpu.CompilerParams(dimension_semantics=("arbitrary",),
                                             vmem_limit_bytes=VMEM_LIMIT),
        name="mixer",
    )(x, x, x, g_mix.reshape(1, d), w_in.astype(jnp.bfloat16), conv_w,
      pool_w.astype(jnp.bfloat16), pool_scale.reshape(1, -1), w_o.astype(jnp.bfloat16))


def _top16(s, payload=None):
    n = s.shape[0]
    rows = lax.broadcasted_iota(jnp.int32, s.shape, 0)
    vals, picks = [], []
    for _ in range(TOPK):
        m = jnp.max(s, axis=0, keepdims=True)
        idx = jnp.min(jnp.where(s == m, rows, n), axis=0, keepdims=True)
        hit = rows == idx
        vals.append(m)
        if payload is None:
            picks.append(idx)
        else:
            picks.append(jnp.max(jnp.where(hit, payload, -1), axis=0, keepdims=True))
        s = jnp.where(hit, -jnp.inf, s)
    return jnp.concatenate(vals, axis=0), jnp.concatenate(picks, axis=0)


_STAIR = [TOPK // (a + 1) for a in range(TOPK)]
_STAIR_ROWS = -(-sum(_STAIR) // SUBLANES) * SUBLANES


def _route_kernel(h_ref, g_ref, wq_ref, keys_ref, xn_ref, idx_ref, gate_ref):
    xn = _rms(h_ref[...], g_ref[...])
    xn_ref[...] = xn
    q = _bdot(xn, wq_ref[...])
    tb = q.shape[0]
    pad = _STAIR_ROWS - sum(_STAIR)
    for h in range(PEER_HEADS):
        sv, si = [], []
        for p in range(2):
            col = (h * 2 + p) * PEER_HALF
            s_t = lax.dot_general(keys_ref[h * 2 + p], q[:, col:col + PEER_HALF],
                                  (((1,), (1,)), ((), ())),
                                  preferred_element_type=jnp.float32)
            v, ix = _top16(s_t)
            sv.append(v)
            si.append(ix)
        cand = jnp.concatenate([sv[0][a:a + 1] + sv[1][0:n] for a, n in enumerate(_STAIR)]
                               + [jnp.full((pad, tb), -jnp.inf, jnp.float32)], axis=0)
        cidx = jnp.concatenate([si[0][a:a + 1] * N_KEYS + si[1][0:n] for a, n in enumerate(_STAIR)]
                               + [jnp.zeros((pad, tb), jnp.int32)], axis=0)
        top_s, eidx = _top16(cand, cidx)
        e = jnp.exp(top_s - top_s[0:1])
        gate = e / jnp.sum(e, axis=0, keepdims=True)
        idx_ref[h * TOPK:(h + 1) * TOPK, :] = eidx * TILE_WORDS
        gate_ref[h * TOPK:(h + 1) * TOPK, :] = gate


def _route(h1, g_ffn, w_q, sub_keys, tb):
    t, d = h1.shape
    keys = sub_keys.reshape(PEER_HEADS * 2, N_KEYS, PEER_HALF)
    full = lambda shape: pl.BlockSpec(shape, lambda i: (0,) * len(shape))
    return pl.pallas_call(
        _route_kernel,
        grid=(t // tb,),
        in_specs=[pl.BlockSpec((tb, d), lambda i: (i, 0)), full((1, d)), full(w_q.shape),
                  full(keys.shape)],
        out_specs=[pl.BlockSpec((tb, d), lambda i: (i, 0)),
                   pl.BlockSpec((PAIRS, tb), lambda i: (0, i)),
                   pl.BlockSpec((PAIRS, tb), lambda i: (0, i))],
        out_shape=[jax.ShapeDtypeStruct((t, d), jnp.float32),
                   jax.ShapeDtypeStruct((PAIRS, t), jnp.int32),
                   jax.ShapeDtypeStruct((PAIRS, t), jnp.float32)],
        compiler_params=pltpu.CompilerParams(dimension_semantics=("arbitrary",),
                                             vmem_limit_bytes=VMEM_LIMIT),
        name="route",
    )(h1, g_ffn.reshape(1, d), w_q.astype(jnp.bfloat16), keys)


def _pack_kernel(w_ref, o_ref):
    o_ref[...] = pltpu.bitcast(w_ref[...].astype(jnp.bfloat16), jnp.uint32)


def _pack_table(w, experts_per_step=1024):
    n, d = w.shape
    assert d == SUBLANES * LANES
    rows = experts_per_step * SUBLANES
    return pl.pallas_call(
        _pack_kernel,
        grid=(n // experts_per_step,),
        in_specs=[pl.BlockSpec((rows, LANES), lambda i: (i, 0))],
        out_specs=pl.BlockSpec((rows // 2, LANES), lambda i: (i, 0)),
        out_shape=jax.ShapeDtypeStruct((n * TILE_WORDS, LANES), jnp.uint32),
        name="pack_table",
    )(w.reshape(n * SUBLANES, LANES))


def _load_tile(tab_ref, offset):
    words = tab_ref[pl.ds(pl.multiple_of(offset, TILE_WORDS), TILE_WORDS), :]
    return pltpu.bitcast(words, jnp.bfloat16).astype(jnp.float32)


def _split_bf16(a):
    hi = a.astype(jnp.bfloat16)
    lo = (a - hi.astype(jnp.float32)).astype(jnp.bfloat16)
    return hi, lo


def _row_sums(a):
    ones = jnp.ones((LANES, LANES), jnp.bfloat16)
    hi, lo = _split_bf16(a)
    return (jnp.dot(hi, ones, preferred_element_type=jnp.float32)
            + jnp.dot(lo, ones, preferred_element_type=jnp.float32))


def _eye():
    r = lax.broadcasted_iota(jnp.int32, (PAIRS, LANES), 0)
    c = lax.broadcasted_iota(jnp.int32, (PAIRS, LANES), 1)
    return r == c


def _fold(a, b, k, sub):
    lo = (sub & k) == 0
    if 2 * k == SUBLANES:
        return jnp.where(lo, a, b) + pltpu.roll(jnp.where(lo, b, a), k, axis=0)
    return (jnp.where(lo, a, pltpu.roll(b, k, axis=0))
            + jnp.where(lo, pltpu.roll(a, SUBLANES - k, axis=0), b))


def _sublane_sums(p, sub):
    q = [_fold(p[0], p[4], 4, sub), _fold(p[2], p[6], 4, sub),
         _fold(p[1], p[5], 4, sub), _fold(p[3], p[7], 4, sub)]
    return _fold(_fold(q[0], q[1], 2, sub), _fold(q[2], q[3], 2, sub), 1, sub)


def _score_kernel(tb, idx_ref, x_ref, gate_ref, tab_ref, c_ref, ra_ref, rb_ref, s_ref):
    sub = lax.broadcasted_iota(jnp.int32, (SUBLANES, LANES), 0)
    eye = _eye()

    def partials(first, r_ref):
        for k in range(GROUP):
            row_ref = idx_ref.at[pl.ds((first + k) * PAIRS, PAIRS)]
            x = x_ref[first + k]
            for g in range(PAIRS // SUBLANES):
                prods = [_load_tile(tab_ref, row_ref[g * SUBLANES + i]) * x
                         for i in range(SUBLANES)]
                r_ref[pl.ds(k * PAIRS + g * SUBLANES, SUBLANES), :] = _sublane_sums(prods, sub)

    def finish(first, z):
        for k in range(GROUP):
            zk = jnp.where(eye, z[k * PAIRS:(k + 1) * PAIRS], 0.0)
            parts = [zk[g * SUBLANES:(g + 1) * SUBLANES] for g in range(PAIRS // SUBLANES)]
            while len(parts) > 1:
                parts = [parts[n] + parts[n + 1] for n in range(0, len(parts), 2)]
            row = pl.multiple_of(jnp.maximum(first + k, 0) * SUBLANES, SUBLANES)
            s_ref[pl.ds(row, SUBLANES), :] = parts[0]

    rb_ref[...] = jnp.zeros((GROUP * PAIRS, LANES), jnp.float32)

    def two_groups(i, carry):
        first = 2 * i * GROUP
        z = _row_sums(rb_ref[...])
        partials(first, ra_ref)
        finish(first - GROUP, z)
        z = _row_sums(ra_ref[...])
        partials(first + GROUP, rb_ref)
        finish(first, z)
        return carry

    lax.fori_loop(0, tb // (2 * GROUP), two_groups, 0)
    finish(tb - GROUP, _row_sums(rb_ref[...]))
    s = jnp.sum(s_ref[...].reshape(tb, SUBLANES, PAIRS), axis=1)
    c_ref[...] = gate_ref[...] * jax.nn.gelu(s)


def _score(idx_flat, xn, gate, table, tb):
    t, d = xn.shape
    x3 = xn.reshape(t, SUBLANES, LANES)
    return pl.pallas_call(
        functools.partial(_score_kernel, tb),
        grid=(t // tb,),
        in_specs=[
            pl.BlockSpec((tb * PAIRS,), lambda i: (i,), memory_space=pltpu.SMEM),
            pl.BlockSpec((tb, SUBLANES, LANES), lambda i: (i, 0, 0)),
            pl.BlockSpec((tb, PAIRS), lambda i: (i, 0)),
            pl.BlockSpec(table.shape, lambda i: (0, 0), pipeline_mode=pl.Buffered(1)),
        ],
        out_specs=pl.BlockSpec((tb, PAIRS), lambda i: (i, 0)),
        out_shape=jax.ShapeDtypeStruct((t, PAIRS), jnp.float32),
        scratch_shapes=[pltpu.VMEM((GROUP * PAIRS, LANES), jnp.float32),
                        pltpu.VMEM((GROUP * PAIRS, LANES), jnp.float32),
                        pltpu.VMEM((tb * SUBLANES, PAIRS), jnp.float32)],
        compiler_params=pltpu.CompilerParams(dimension_semantics=("arbitrary",),
                                             vmem_limit_bytes=VMEM_LIMIT),
        name="score",
    )(idx_flat, x3, gate, table)


_ACCS = 2
CGROUP = 8


def _combine_kernel(tb, idx_ref, c_ref, tab_ref, o_ref, cba_ref, cbb_ref):
    eye = _eye()

    def spread(first):
        rows = [jnp.where(eye, jnp.broadcast_to(c_ref[pl.ds(jnp.minimum(first + k, tb - 1), 1), :],
                                                (PAIRS, LANES)), 0.0) for k in range(CGROUP)]
        return _row_sums(jnp.concatenate(rows, axis=0))

    def accumulate(first, cb_ref):
        row_refs = [idx_ref.at[pl.ds((first + k) * PAIRS, PAIRS)] for k in range(CGROUP)]
        accs = [[jnp.zeros((SUBLANES, LANES), jnp.float32) for _ in range(_ACCS)]
                for _ in range(CGROUP)]
        for j in range(PAIRS):
            for k in range(CGROUP):
                accs[k][j % _ACCS] = accs[k][j % _ACCS] + (
                    cb_ref[pl.ds(k * PAIRS + j, 1), :] * _load_tile(tab_ref, row_refs[k][j]))
        for k in range(CGROUP):
            total = accs[k][0]
            for a in accs[k][1:]:
                total = total + a
            o_ref[first + k] = total

    cba_ref[...] = spread(0)

    def two_groups(i, carry):
        first = 2 * i * CGROUP
        nxt = spread(first + CGROUP)
        accumulate(first, cba_ref)
        cbb_ref[...] = nxt
        nxt = spread(first + 2 * CGROUP)
        accumulate(first + CGROUP, cbb_ref)
        cba_ref[...] = nxt
        return carry

    lax.fori_loop(0, tb // (2 * CGROUP), two_groups, 0)


def _combine(idx_flat, coef, table, tb):
    t = coef.shape[0]
    return pl.pallas_call(
        functools.partial(_combine_kernel, tb),
        grid=(t // tb,),
        in_specs=[
            pl.BlockSpec((tb * PAIRS,), lambda i: (i,), memory_space=pltpu.SMEM),
            pl.BlockSpec((tb, PAIRS), lambda i: (i, 0)),
            pl.BlockSpec(table.shape, lambda i: (0, 0), pipeline_mode=pl.Buffered(1)),
        ],
        out_specs=pl.BlockSpec((tb, SUBLANES, LANES), lambda i: (i, 0, 0)),
        out_shape=jax.ShapeDtypeStruct((t, SUBLANES, LANES), jnp.float32),
        scratch_shapes=[pltpu.VMEM((CGROUP * PAIRS, LANES), jnp.float32),
                        pltpu.VMEM((CGROUP * PAIRS, LANES), jnp.float32)],
        compiler_params=pltpu.CompilerParams(dimension_semantics=("arbitrary",),
                                             vmem_limit_bytes=VMEM_LIMIT),
        name="combine",
    )(idx_flat, coef, table)


def _ple_kernel(final_norm, h_ref, y_ref, p_ref, g_ref, wg_ref, wp_ref, gf_ref, o_ref):
    h = h_ref[...] + y_ref[...]
    gate = jax.nn.sigmoid(_bdot(_rms(h, g_ref[...]), wg_ref[...]))
    h = h + gate * _bdot(p_ref[...], wp_ref[...])
    o_ref[...] = _rms(h, gf_ref[...]) if final_norm else h


def _ple(h1, y, p, g_ple, w_gate, w_proj, g_final, final_norm, tb):
    t, d = h1.shape
    full = lambda shape: pl.BlockSpec(shape, lambda i: (0,) * len(shape))
    return pl.pallas_call(
        functools.partial(_ple_kernel, final_norm),
        grid=(t // tb,),
        in_specs=[pl.BlockSpec((tb, d), lambda i: (i, 0)), pl.BlockSpec((tb, d), lambda i: (i, 0)),
                  pl.BlockSpec((tb, p.shape[-1]), lambda i: (i, 0)), full((1, d)),
                  full(w_gate.shape), full(w_proj.shape), full((1, d))],
        out_specs=pl.BlockSpec((tb, d), lambda i: (i, 0)),
        out_shape=jax.ShapeDtypeStruct((t, d), jnp.float32),
        compiler_params=pltpu.CompilerParams(dimension_semantics=("arbitrary",),
                                             vmem_limit_bytes=VMEM_LIMIT),
        name="ple",
    )(h1, y, p, g_ple.reshape(1, d), w_gate.astype(jnp.bfloat16), w_proj.astype(jnp.bfloat16),
      g_final.reshape(1, d))


def kernel(x, p, g_mix, w_in, conv_w, pool_w, pool_scale, w_o, g_ffn, w_q, sub_keys, expert_u,
           expert_v, g_ple, w_ple_gate, w_ple_proj, g_final):
    bsz, seq, d = x.shape
    t = bsz * seq
    h = x.reshape(t, d)
    for i in range(p.shape[0]):
        h1 = _mixer(h, g_mix[i], w_in[i], conv_w[i], pool_w[i], pool_scale[i], w_o[i], seq,
                    tb=min(512, seq))
        xn, idx_t, gate_t = _route(h1, g_ffn[i], w_q[i], sub_keys[i], tb=min(128, t))
        idx = idx_t.T.reshape(t * PAIRS)
        coef = _score(idx, xn, gate_t.T, _pack_table(expert_u[i]), tb=min(128, t))
        y = _combine(idx, coef, _pack_table(expert_v[i]), tb=min(128, t))
        h = _ple(h1, y.reshape(t, d), p[i].reshape(t, -1), g_ple[i], w_ple_gate[i],
                 w_ple_proj[i], g_final, final_norm=(i == p.shape[0] - 1), tb=min(512, t))
    return h.reshape(bsz, seq, d)
```

```python
import functools

import jax
import jax.numpy as jnp
from jax import lax
from jax.experimental import pallas as pl
from jax.experimental.pallas import tpu as pltpu

EPS = 1e-6
CONV_WIDTH = 512
POOL_WINDOWS = (2, 4, 8, 16)
POOL_GROUP = 128
PEER_HEADS = 8
PEER_HALF = 128
N_KEYS = 128
TOPK = 16
HALO = 8
PAIRS = PEER_HEADS * TOPK
LANES = 128
SUBLANES = 8
TILE_WORDS = SUBLANES // 2
GROUP = 8
VMEM_LIMIT = 56 * 1024 * 1024


def _rms(x, g):
    return x * lax.rsqrt(jnp.mean(x * x, axis=-1, keepdims=True) + EPS) * g


def _bdot(a, b):
    return jnp.dot(a.astype(jnp.bfloat16), b.astype(jnp.bfloat16),
                   preferred_element_type=jnp.float32)


def _mixer_kernel(seq, tb, xp_ref, x_ref, xn_ref, g_ref, win_ref, cw_ref, pw_ref, ps_ref,
                  wo_ref, o_ref):
    i = pl.program_id(0)
    pos0 = (i * tb) % seq
    x = x_ref[...]
    xcat = jnp.concatenate([xp_ref[...], x, xn_ref[...]], axis=0)
    proj = _bdot(_rms(xcat, g_ref[...]), win_ref[...])
    rows = lax.broadcasted_iota(jnp.int32, (tb + 2 * HALO, 1), 0)
    pos = pos0 - HALO + rows
    inside = (pos >= 0) & (pos < seq)
    c = CONV_WIDTH
    b_gate = proj[HALO:HALO + tb, 0:c]
    z = jnp.where(inside, proj[:, c:2 * c] * proj[:, 2 * c:3 * c], 0.0)
    cw = cw_ref[...]
    y_conv = b_gate * (cw[0:1] * z[HALO - 1:HALO - 1 + tb] + cw[1:2] * z[HALO:HALO + tb]
                       + cw[2:3] * z[HALO + 1:HALO + 1 + tb])
    u = jnp.where(inside, proj[:, 3 * c:4 * c], 0.0)
    tpos = pos[HALO:HALO + tb]
    pooled = []
    for g, w in enumerate(POOL_WINDOWS):
        r = w // 2
        ug = u[:, g * POOL_GROUP:(g + 1) * POOL_GROUP]
        win = ug[HALO - r:HALO - r + tb]
        for k in range(-r + 1, r):
            win = win + ug[HALO + k:HALO + k + tb]
        cnt = (jnp.minimum(tpos + r, seq) - jnp.maximum(tpos - r, 0)).astype(jnp.float32)
        pooled_g = win / cnt - ug[HALO:HALO + tb]
        pooled.append(_bdot(pooled_g, pw_ref[g]))
    y_pool = jnp.concatenate(pooled, axis=-1) * ps_ref[...]
    y = jnp.concatenate([y_conv, y_pool], axis=-1)
    o_ref[...] = x + _bdot(y, wo_ref[...])


def _mixer(x, g_mix, w_in, conv_w, pool_w, pool_scale, w_o, seq, tb):
    t, d = x.shape
    nb = tb // HALO
    last = t // HALO - 1
    full = lambda shape: pl.BlockSpec(shape, lambda i: (0,) * len(shape))
    return pl.pallas_call(
        functools.partial(_mixer_kernel, seq, tb),
        grid=(t // tb,),
        in_specs=[
            pl.BlockSpec((HALO, d), lambda i: (jnp.maximum(i * nb - 1, 0), 0)),
            pl.BlockSpec((tb, d), lambda i: (i, 0)),
            pl.BlockSpec((HALO, d), lambda i: (jnp.minimum((i + 1) * nb, last), 0)),
            full((1, d)), full(w_in.shape), full(conv_w.shape), full(pool_w.shape),
            full((1, pool_scale.shape[-1])), full(w_o.shape),
        ],
        out_specs=pl.BlockSpec((tb, d), lambda i: (i, 0)),
        out_shape=jax.ShapeDtypeStruct((t, d), jnp.float32),
        compiler_params=pltpu.CompilerParams(dimension_semantics=("arbitrary",),
                                             vmem_limit_bytes=VMEM_LIMIT),
        name="mixer",
    )(x, x, x, g_mix.reshape(1, d), w_in.astype(jnp.bfloat16), conv_w,
      pool_w.astype(jnp.bfloat16), pool_scale.reshape(1, -1), w_o.astype(jnp.bfloat16))


def _top16(s, payload=None):
    n = s.shape[0]
    rows = lax.broadcasted_iota(jnp.int32, s.shape, 0)
    vals, picks = [], []
    for _ in range(TOPK):
        m = jnp.max(s, axis=0, keepdims=True)
        idx = jnp.min(jnp.where(s == m, rows, n), axis=0, keepdims=True)
        hit = rows == idx
        vals.append(m)
        if payload is None:
            picks.append(idx)
        else:
            picks.append(jnp.max(jnp.where(hit, payload, -1), axis=0, keepdims=True))
        s = jnp.where(hit, -jnp.inf, s)
    return jnp.concatenate(vals, axis=0), jnp.concatenate(picks, axis=0)


_STAIR = [TOPK // (a + 1) for a in range(TOPK)]
_STAIR_ROWS = -(-sum(_STAIR) // SUBLANES) * SUBLANES


def _route_kernel(h_ref, g_ref, wq_ref, keys_ref, xn_ref, idx_ref, gate_ref):
    xn = _rms(h_ref[...], g_ref[...])
    xn_ref[...] = xn
    q = _bdot(xn, wq_ref[...])
    tb = q.shape[0]
    pad = _STAIR_ROWS - sum(_STAIR)
    offsets, gates = [], []
    for h in range(PEER_HEADS):
        sv, si = [], []
        for p in range(2):
            col = (h * 2 + p) * PEER_HALF
            s_t = lax.dot_general(keys_ref[h * 2 + p], q[:, col:col + PEER_HALF],
                                  (((1,), (1,)), ((), ())),
                                  preferred_element_type=jnp.float32)
            v, ix = _top16(s_t)
            sv.append(v)
            si.append(ix)
        cand = jnp.concatenate([sv[0][a:a + 1] + sv[1][0:n] for a, n in enumerate(_STAIR)]
                               + [jnp.full((pad, tb), -jnp.inf, jnp.float32)], axis=0)
        cidx = jnp.concatenate([si[0][a:a + 1] * N_KEYS + si[1][0:n] for a, n in enumerate(_STAIR)]
                               + [jnp.zeros((pad, tb), jnp.int32)], axis=0)
        top_s, eidx = _top16(cand, cidx)
        e = jnp.exp(top_s - top_s[0:1])
        gate = e / jnp.sum(e, axis=0, keepdims=True)
        offsets.append(eidx * TILE_WORDS)
        gates.append(gate)
    idx_ref[...] = jnp.concatenate(offsets, axis=0).T
    gate_ref[...] = jnp.concatenate(gates, axis=0).T


def _route(h1, g_ffn, w_q, sub_keys, tb):
    t, d = h1.shape
    keys = sub_keys.reshape(PEER_HEADS * 2, N_KEYS, PEER_HALF)
    full = lambda shape: pl.BlockSpec(shape, lambda i: (0,) * len(shape))
    return pl.pallas_call(
        _route_kernel,
        grid=(t // tb,),
        in_specs=[pl.BlockSpec((tb, d), lambda i: (i, 0)), full((1, d)), full(w_q.shape),
                  full(keys.shape)],
        out_specs=[pl.BlockSpec((tb, d), lambda i: (i, 0)),
                   pl.BlockSpec((tb, PAIRS), lambda i: (i, 0)),
                   pl.BlockSpec((tb, PAIRS), lambda i: (i, 0))],
        out_shape=[jax.ShapeDtypeStruct((t, d), jnp.float32),
                   jax.ShapeDtypeStruct((t, PAIRS), jnp.int32),
                   jax.ShapeDtypeStruct((t, PAIRS), jnp.float32)],
        compiler_params=pltpu.CompilerParams(dimension_semantics=("arbitrary",),
                                             vmem_limit_bytes=VMEM_LIMIT),
        name="route",
    )(h1, g_ffn.reshape(1, d), w_q.astype(jnp.bfloat16), keys)


def _pack_kernel(w_ref, o_ref, t_ref):
    experts = w_ref.shape[0]
    for s in range(SUBLANES):
        t_ref[pl.ds(s, experts, stride=SUBLANES), :] = w_ref[:, s * LANES:(s + 1) * LANES]
    o_ref[...] = pltpu.bitcast(t_ref[...].astype(jnp.bfloat16), jnp.uint32)


def _pack_table(w, experts_per_step=512):
    n, d = w.shape
    assert d == SUBLANES * LANES
    rows = experts_per_step * SUBLANES
    return pl.pallas_call(
        _pack_kernel,
        grid=(n // experts_per_step,),
        in_specs=[pl.BlockSpec((experts_per_step, d), lambda i: (i, 0))],
        out_specs=pl.BlockSpec((rows // 2, LANES), lambda i: (i, 0)),
        out_shape=jax.ShapeDtypeStruct((n * TILE_WORDS, LANES), jnp.uint32),
        scratch_shapes=[pltpu.VMEM((rows, LANES), jnp.float32)],
        name="pack_table",
    )(w)


def _load_tile(tab_ref, offset):
    words = tab_ref[pl.ds(pl.multiple_of(offset, TILE_WORDS), TILE_WORDS), :]
    return pltpu.bitcast(words, jnp.bfloat16).astype(jnp.float32)


def _split_bf16(a):
    hi = a.astype(jnp.bfloat16)
    lo = (a - hi.astype(jnp.float32)).astype(jnp.bfloat16)
    return hi, lo


def _row_sums(a):
    ones = jnp.ones((LANES, LANES), jnp.bfloat16)
    hi, lo = _split_bf16(a)
    return (jnp.dot(hi, ones, preferred_element_type=jnp.float32)
            + jnp.dot(lo, ones, preferred_element_type=jnp.float32))


def _eye():
    r = lax.broadcasted_iota(jnp.int32, (PAIRS, LANES), 0)
    c = lax.broadcasted_iota(jnp.int32, (PAIRS, LANES), 1)
    return r == c


def _fold(a, b, k, rows, sub):
    assert all(((r - k) % SUBLANES in rows) != (r in rows) for r in range(SUBLANES))
    m = functools.reduce(jnp.logical_or, [sub == r for r in rows])
    return jnp.where(m, a, b) + pltpu.roll(jnp.where(m, b, a), k, axis=0)


def _sublane_sums(p, sub):
    a = _fold(p[1], p[5], 4, (0, 1, 6, 7), sub)
    b = _fold(p[3], p[7], 4, (0, 1, 2, 3), sub)
    c = _fold(a, b, 2, (0, 1, 4, 5), sub)
    a = _fold(p[2], p[6], 4, (7, 0, 1, 2), sub)
    b = _fold(p[0], p[4], 4, (5, 6, 7, 0), sub)
    d = _fold(a, b, 2, (1, 2, 5, 6), sub)
    return _fold(c, d, 1, (1, 3, 5, 7), sub)


def _score_kernel(tb, idx_ref, x_ref, gate_ref, tab_ref, c_ref, ra_ref, rb_ref, s_ref):
    sub = lax.broadcasted_iota(jnp.int32, (SUBLANES, LANES), 0)
    eye = _eye()

    def partials(first, r_ref):
        for k in range(GROUP):
            row_ref = idx_ref.at[pl.ds((first + k) * PAIRS, PAIRS)]
            x = x_ref[first + k]
            for g in range(PAIRS // SUBLANES):
                prods = [_load_tile(tab_ref, row_ref[g * SUBLANES + i]) * x
                         for i in range(SUBLANES)]
                r_ref[pl.ds(k * PAIRS + g * SUBLANES, SUBLANES), :] = _sublane_sums(prods, sub)

    def finish(first, z):
        for k in range(GROUP):
            zk = jnp.where(eye, z[k * PAIRS:(k + 1) * PAIRS], 0.0)
            parts = [zk[g * SUBLANES:(g + 1) * SUBLANES] for g in range(PAIRS // SUBLANES)]
            while len(parts) > 1:
                parts = [parts[n] + parts[n + 1] for n in range(0, len(parts), 2)]
            row = pl.multiple_of(jnp.maximum(first + k, 0) * SUBLANES, SUBLANES)
            s_ref[pl.ds(row, SUBLANES), :] = parts[0]

    rb_ref[...] = jnp.zeros((GROUP * PAIRS, LANES), jnp.float32)

    def two_groups(i, carry):
        first = 2 * i * GROUP
        z = _row_sums(rb_ref[...])
        partials(first, ra_ref)
        finish(first - GROUP, z)
        z = _row_sums(ra_ref[...])
        partials(first + GROUP, rb_ref)
        finish(first, z)
        return carry

    lax.fori_loop(0, tb // (2 * GROUP), two_groups, 0)
    finish(tb - GROUP, _row_sums(rb_ref[...]))
    s = jnp.sum(s_ref[...].reshape(tb, SUBLANES, PAIRS), axis=1)
    c_ref[...] = gate_ref[...] * jax.nn.gelu(s)


def _score(idx_flat, xn, gate, table, tb):
    t, d = xn.shape
    x3 = xn.reshape(t, SUBLANES, LANES)
    return pl.pallas_call(
        functools.partial(_score_kernel, tb),
        grid=(t // tb,),
        in_specs=[
            pl.BlockSpec((tb * PAIRS,), lambda i: (i,), memory_space=pltpu.SMEM),
            pl.BlockSpec((tb, SUBLANES, LANES), lambda i: (i, 0, 0)),
            pl.BlockSpec((tb, PAIRS), lambda i: (i, 0)),
            pl.BlockSpec(table.shape, lambda i: (0, 0), pipeline_mode=pl.Buffered(1)),
        ],
        out_specs=pl.BlockSpec((tb, PAIRS), lambda i: (i, 0)),
        out_shape=jax.ShapeDtypeStruct((t, PAIRS), jnp.float32),
        scratch_shapes=[pltpu.VMEM((GROUP * PAIRS, LANES), jnp.float32),
                        pltpu.VMEM((GROUP * PAIRS, LANES), jnp.float32),
                        pltpu.VMEM((tb * SUBLANES, PAIRS), jnp.float32)],
        compiler_params=pltpu.CompilerParams(dimension_semantics=("arbitrary",),
                                             vmem_limit_bytes=VMEM_LIMIT),
        name="score",
    )(idx_flat, x3, gate, table)


_ACCS = 2
CGROUP = 8


def _combine_kernel(tb, idx_ref, c_ref, tab_ref, o_ref, cba_ref, cbb_ref):
    eye = _eye()

    def spread(first):
        rows = [jnp.where(eye, jnp.broadcast_to(c_ref[pl.ds(jnp.minimum(first + k, tb - 1), 1), :],
                                                (PAIRS, LANES)), 0.0) for k in range(CGROUP)]
        return _row_sums(jnp.concatenate(rows, axis=0))

    def accumulate(first, cb_ref):
        row_refs = [idx_ref.at[pl.ds((first + k) * PAIRS, PAIRS)] for k in range(CGROUP)]
        accs = [[jnp.zeros((SUBLANES, LANES), jnp.float32) for _ in range(_ACCS)]
                for _ in range(CGROUP)]
        for j in range(PAIRS):
            for k in range(CGROUP):
                accs[k][j % _ACCS] = accs[k][j % _ACCS] + (
                    cb_ref[pl.ds(k * PAIRS + j, 1), :] * _load_tile(tab_ref, row_refs[k][j]))
        for k in range(CGROUP):
            total = accs[k][0]
            for a in accs[k][1:]:
                total = total + a
            o_ref[first + k] = total

    cba_ref[...] = spread(0)

    def two_groups(i, carry):
        first = 2 * i * CGROUP
        nxt = spread(first + CGROUP)
        accumulate(first, cba_ref)
        cbb_ref[...] = nxt
        nxt = spread(first + 2 * CGROUP)
        accumulate(first + CGROUP, cbb_ref)
        cba_ref[...] = nxt
        return carry

    lax.fori_loop(0, tb // (2 * CGROUP), two_groups, 0)


def _combine(idx_flat, coef, table, tb):
    t = coef.shape[0]
    return pl.pallas_call(
        functools.partial(_combine_kernel, tb),
        grid=(t // tb,),
        in_specs=[
            pl.BlockSpec((tb * PAIRS,), lambda i: (i,), memory_space=pltpu.SMEM),
            pl.BlockSpec((tb, PAIRS), lambda i: (i, 0)),
            pl.BlockSpec(table.shape, lambda i: (0, 0), pipeline_mode=pl.Buffered(1)),
        ],
        out_specs=pl.BlockSpec((tb, SUBLANES, LANES), lambda i: (i, 0, 0)),
        out_shape=jax.ShapeDtypeStruct((t, SUBLANES, LANES), jnp.float32),
        scratch_shapes=[pltpu.VMEM((CGROUP * PAIRS, LANES), jnp.float32),
                        pltpu.VMEM((CGROUP * PAIRS, LANES), jnp.float32)],
        compiler_params=pltpu.CompilerParams(dimension_semantics=("arbitrary",),
                                             vmem_limit_bytes=VMEM_LIMIT),
        name="combine",
    )(idx_flat, coef, table)


def _ple_kernel(final_norm, h_ref, y_ref, p_ref, g_ref, wg_ref, wp_ref, gf_ref, o_ref):
    h = h_ref[...] + y_ref[...]
    gate = jax.nn.sigmoid(_bdot(_rms(h, g_ref[...]), wg_ref[...]))
    h = h + gate * _bdot(p_ref[...], wp_ref[...])
    o_ref[...] = _rms(h, gf_ref[...]) if final_norm else h


def _ple(h1, y, p, g_ple, w_gate, w_proj, g_final, final_norm, tb):
    t, d = h1.shape
    full = lambda shape: pl.BlockSpec(shape, lambda i: (0,) * len(shape))
    return pl.pallas_call(
        functools.partial(_ple_kernel, final_norm),
        grid=(t // tb,),
        in_specs=[pl.BlockSpec((tb, d), lambda i: (i, 0)), pl.BlockSpec((tb, d), lambda i: (i, 0)),
                  pl.BlockSpec((tb, p.shape[-1]), lambda i: (i, 0)), full((1, d)),
                  full(w_gate.shape), full(w_proj.shape), full((1, d))],
        out_specs=pl.BlockSpec((tb, d), lambda i: (i, 0)),
        out_shape=jax.ShapeDtypeStruct((t, d), jnp.float32),
        compiler_params=pltpu.CompilerParams(dimension_semantics=("arbitrary",),
                                             vmem_limit_bytes=VMEM_LIMIT),
        name="ple",
    )(h1, y, p, g_ple.reshape(1, d), w_gate.astype(jnp.bfloat16), w_proj.astype(jnp.bfloat16),
      g_final.reshape(1, d))


def kernel(x, p, g_mix, w_in, conv_w, pool_w, pool_scale, w_o, g_ffn, w_q, sub_keys, expert_u,
           expert_v, g_ple, w_ple_gate, w_ple_proj, g_final):
    bsz, seq, d = x.shape
    t = bsz * seq
    h = x.reshape(t, d)
    for i in range(p.shape[0]):
        h1 = _mixer(h, g_mix[i], w_in[i], conv_w[i], pool_w[i], pool_scale[i], w_o[i], seq,
                    tb=min(512, seq))
        xn, idx, gate = _route(h1, g_ffn[i], w_q[i], sub_keys[i], tb=min(128, t))
        idx = idx.reshape(t * PAIRS)
        coef = _score(idx, xn, gate, _pack_table(expert_u[i]), tb=min(128, t))
        y = _combine(idx, coef, _pack_table(expert_v[i]), tb=min(128, t))
        h = _ple(h1, y.reshape(t, d), p[i].reshape(t, -1), g_ple[i], w_ple_gate[i],
                 w_ple_proj[i], g_final, final_norm=(i == p.shape[0] - 1), tb=min(512, t))
    return h.reshape(bsz, seq, d)
```
